```python
import jax, jax.numpy as jnp
from jax import lax
import numpy as np

D_MODEL = 1024
BATCH = 8
SEQ = 2048
DEPTH = 2
DEC_BATCH = 128
DEC_SEQ = 8
PAST_LEN = 2048
PAGE_SIZE = 128

CONV_DIM = D_MODEL // 2
CONV_WIDTH = 31
SB_HEADS = 8
SB_HEAD_DIM = 64
SB_DIM = SB_HEADS * SB_HEAD_DIM
SB_BIAS_INIT = -6.0
Q_BLOCK = 128
SC_DIM = D_MODEL // 2
SC_WIDTH = 3
POOL_DIM = D_MODEL // 2
POOL_WINDOWS = (2, 4, 8, 16)
POOL_GROUP = POOL_DIM // len(POOL_WINDOWS)
POOL_MAX = max(POOL_WINDOWS)
FFN_DIM = ((-(-8 * D_MODEL // 3) + 255) // 256) * 256
N_EVEN = (DEPTH + 1) // 2
N_ODD = DEPTH // 2
EPS = 1e-6

kernel_name = "hybrid_conformer_stickbreak_shortconv_pool_decode_step"


def rms_norm(x, g):
    xf = x.astype(jnp.float32)
    y = xf * lax.rsqrt(jnp.mean(xf * xf, axis=-1, keepdims=True) + EPS)
    return (y * g.astype(jnp.float32)).astype(x.dtype)


def layer_norm(x, g, b):
    xf = x.astype(jnp.float32)
    mu = jnp.mean(xf, axis=-1, keepdims=True)
    var = jnp.mean(jnp.square(xf - mu), axis=-1, keepdims=True)
    y = (xf - mu) * lax.rsqrt(var + EPS) * g.astype(jnp.float32) + b.astype(jnp.float32)
    return y.astype(x.dtype)


def causal_depthwise_conv(ext, w):
    c = ext.shape[-1]
    return lax.conv_general_dilated(ext, w[:, None, :].astype(ext.dtype), window_strides=(1,), padding='VALID',
                                    dimension_numbers=('NWC', 'WIO', 'NWC'), feature_group_count=c)


def stick_breaking_attention(q, k, v, bias, q_pos, k_pos):
    n, lq, h, dh = q.shape
    qb = min(Q_BLOCK, lq)
    nb = lq // qb
    scale = dh ** -0.5
    qs = q.reshape(n, nb, qb, h, dh).transpose(1, 0, 2, 3, 4)
    ps = q_pos.reshape(nb, qb)
    bias_f = bias.astype(jnp.float32)[None, :, None, None]

    def block(args):
        qblk, pblk = args
        z = jnp.einsum('nqhd,nkhd->nhqk', qblk, k, preferred_element_type=jnp.float32) * scale + bias_f
        mask = k_pos[None, :] < pblk[:, None]
        log_1m = jnp.where(mask, jax.nn.log_sigmoid(-z), 0.0)
        after = lax.cumsum(log_1m, axis=3, reverse=True) - log_1m
        logw = jnp.where(mask, jax.nn.log_sigmoid(z) + after, -jnp.inf)
        wgt = jnp.exp(logw)
        return jnp.einsum('nhqk,nkhd->nqhd', wgt.astype(v.dtype), v)

    out = lax.map(block, (qs, ps))
    return out.transpose(1, 0, 2, 3, 4).reshape(n, lq, h * dh)


def conformer_sb_mixer(hn, conv_prev, k_past, v_past, start, w_in, conv_w, conv_b, ln_g, ln_b, sb_bias, w_out):
    n, L, _ = hn.shape
    proj = hn @ w_in
    a_val, a_gate, q, k, v = jnp.split(
        proj, [CONV_DIM, 2 * CONV_DIM, 2 * CONV_DIM + SB_DIM, 2 * CONV_DIM + 2 * SB_DIM], axis=-1)
    glu = a_val * jax.nn.sigmoid(a_gate)
    ext = jnp.concatenate([conv_prev, glu], axis=1)
    a = causal_depthwise_conv(ext, conv_w) + conv_b
    a = jax.nn.silu(layer_norm(a, ln_g, ln_b))
    new_conv = ext[:, -(CONV_WIDTH - 1):]
    q = q.reshape(n, L, SB_HEADS, SB_HEAD_DIM)
    k = k.reshape(n, L, SB_HEADS, SB_HEAD_DIM)
    v = v.reshape(n, L, SB_HEADS, SB_HEAD_DIM)
    k_all = jnp.concatenate([k_past, k], axis=1)
    v_all = jnp.concatenate([v_past, v], axis=1)
    q_pos = start + jnp.arange(L, dtype=jnp.int32)
    k_pos = jnp.arange(k_all.shape[1], dtype=jnp.int32)
    b = stick_breaking_attention(q, k_all, v_all, sb_bias, q_pos, k_pos)
    out = jnp.concatenate([a, b], axis=-1) @ w_out
    return out, new_conv, k, v


def shortconv_pool_mixer(hn, sc_prev, pool_prev, start, w_in, sc_w, pool_w, pool_scale, w_out):
    n, L, _ = hn.shape
    proj = hn @ w_in
    gb, gc, xv, u = jnp.split(proj, [SC_DIM, 2 * SC_DIM, 3 * SC_DIM], axis=-1)
    ext = jnp.concatenate([sc_prev, gc * xv], axis=1)
    c = gb * causal_depthwise_conv(ext, sc_w)
    new_sc = ext[:, -(SC_WIDTH - 1):]
    p = POOL_MAX - 1
    pext = jnp.concatenate([pool_prev, u], axis=1)
    cs = jnp.pad(jnp.cumsum(pext.astype(jnp.float32), axis=1), ((0, 0), (1, 0), (0, 0)))
    pos = start + jnp.arange(L, dtype=jnp.int32)
    groups = []
    for gi, w in enumerate(POOL_WINDOWS):
        lo, hi = gi * POOL_GROUP, (gi + 1) * POOL_GROUP
        wsum = cs[:, p + 1:p + 1 + L, lo:hi] - cs[:, p + 1 - w:p + 1 - w + L, lo:hi]
        cnt = jnp.minimum(w, pos + 1).astype(jnp.float32)[None, :, None]
        groups.append(wsum / cnt)
    pooled = jnp.concatenate(groups, axis=-1).astype(u.dtype) - u
    pooled = pooled.reshape(n, L, len(POOL_WINDOWS), POOL_GROUP)
    d = jnp.einsum('nlgc,gcd->nlgd', pooled, pool_w).reshape(n, L, POOL_DIM) * pool_scale
    new_pool = pext[:, -p:]
    out = jnp.concatenate([c, d], axis=-1) @ w_out
    return out, new_sc, new_pool


def swiglu(h, w_in, w_out):
    g, u = jnp.split(h @ w_in, 2, axis=-1)
    return (jax.nn.silu(g) * u) @ w_out


def setup_inputs(seed: int = 0) -> dict:
    key = jax.random.key(seed)
    ks = jax.random.split(key, 32)
    n_pages = PAST_LEN // PAGE_SIZE
    n_pool = (5 * DEC_BATCH * n_pages + 3) // 4
    f32 = jnp.float32
    nrm = lambda k, shape, s: (jax.random.normal(k, shape, f32) * s).astype(f32)
    perm = jax.random.permutation(ks[7], n_pool)
    page_table = perm[:DEC_BATCH * n_pages].reshape(DEC_BATCH, n_pages).astype(jnp.int32)
    e_in = 2 * CONV_DIM + 3 * SB_DIM
    o_in = 3 * SC_DIM + POOL_DIM
    return {
        "x_prompt": nrm(ks[0], (BATCH, SEQ, D_MODEL), 1.0),
        "x_sample": nrm(ks[1], (DEC_BATCH, DEC_SEQ, D_MODEL), 1.0),
        "cache_k": nrm(ks[2], (N_EVEN, n_pool, PAGE_SIZE, SB_HEADS, SB_HEAD_DIM), 1.0),
        "cache_v": nrm(ks[3], (N_EVEN, n_pool, PAGE_SIZE, SB_HEADS, SB_HEAD_DIM), 1.0),
        "state_conformer": nrm(ks[4], (N_EVEN, DEC_BATCH, CONV_WIDTH - 1, CONV_DIM), 0.5),
        "state_shortconv": nrm(ks[5], (N_ODD, DEC_BATCH, SC_WIDTH - 1, SC_DIM), 1.0),
        "state_pool": nrm(ks[6], (N_ODD, DEC_BATCH, POOL_MAX - 1, POOL_DIM), 1.0),
        "page_table": page_table,
        "norm_mix_g": 1.0 + nrm(ks[8], (DEPTH, D_MODEL), 0.02),
        "norm_ffn_g": 1.0 + nrm(ks[9], (DEPTH, D_MODEL), 0.02),
        "norm_final_g": 1.0 + nrm(ks[10], (D_MODEL,), 0.02),
        "w_in_even": nrm(ks[11], (N_EVEN, D_MODEL, e_in), D_MODEL ** -0.5),
        "conv_a_w": nrm(ks[12], (N_EVEN, CONV_WIDTH, CONV_DIM), CONV_WIDTH ** -0.5),
        "conv_a_b": nrm(ks[13], (N_EVEN, CONV_DIM), 0.01),
        "ln_a_g": 1.0 + nrm(ks[14], (N_EVEN, CONV_DIM), 0.02),
        "ln_a_b": nrm(ks[15], (N_EVEN, CONV_DIM), 0.01),
        "sb_bias": SB_BIAS_INIT + nrm(ks[24], (N_EVEN, SB_HEADS), 0.5),
        "w_out_even": nrm(ks[16], (N_EVEN, CONV_DIM + SB_DIM, D_MODEL), (CONV_DIM + SB_DIM) ** -0.5),
        "w_in_odd": nrm(ks[17], (N_ODD, D_MODEL, o_in), D_MODEL ** -0.5),
        "conv_c_w": nrm(ks[18], (N_ODD, SC_WIDTH, SC_DIM), SC_WIDTH ** -0.5),
        "pool_w": nrm(ks[19], (N_ODD, len(POOL_WINDOWS), POOL_GROUP, POOL_GROUP), POOL_GROUP ** -0.5),
        "pool_scale": 1.0 + nrm(ks[20], (N_ODD, POOL_DIM), 0.1),
        "w_out_odd": nrm(ks[21], (N_ODD, SC_DIM + POOL_DIM, D_MODEL), (SC_DIM + POOL_DIM) ** -0.5),
        "w_ffn_in": nrm(ks[22], (DEPTH, D_MODEL, 2 * FFN_DIM), D_MODEL ** -0.5),
        "w_ffn_out": nrm(ks[23], (DEPTH, FFN_DIM, D_MODEL), FFN_DIM ** -0.5),
    }


def reference(x_prompt, x_sample, cache_k, cache_v, state_conformer, state_shortconv, state_pool, page_table,
              norm_mix_g, norm_ffn_g, norm_final_g, w_in_even, conv_a_w, conv_a_b, ln_a_g, ln_a_b, sb_bias,
              w_out_even, w_in_odd, conv_c_w, pool_w, pool_scale, w_out_odd, w_ffn_in, w_ffn_out):

    def trunk(x, start, conv_prev, k_past, v_past, sc_prev, pool_prev):
        h = x
        new_conv, new_k, new_v, new_sc, new_pool = [], [], [], [], []
        for layer in range(DEPTH):
            i = layer // 2
            hn = rms_norm(h, norm_mix_g[layer])
            if layer % 2 == 0:
                mix, c_new, k_new, v_new = conformer_sb_mixer(
                    hn, conv_prev(i), k_past(i), v_past(i), start, w_in_even[i], conv_a_w[i], conv_a_b[i],
                    ln_a_g[i], ln_a_b[i], sb_bias[i], w_out_even[i])
                new_conv.append(c_new)
                new_k.append(k_new)
                new_v.append(v_new)
            else:
                mix, s_new, p_new = shortconv_pool_mixer(
                    hn, sc_prev(i), pool_prev(i), start, w_in_odd[i], conv_c_w[i], pool_w[i], pool_scale[i],
                    w_out_odd[i])
                new_sc.append(s_new)
                new_pool.append(p_new)
            h = h + mix
            h = h + swiglu(rms_norm(h, norm_ffn_g[layer]), w_ffn_in[layer], w_ffn_out[layer])
        y = rms_norm(h, norm_final_g)
        return (y, jnp.stack(new_k), jnp.stack(new_v), jnp.stack(new_conv), jnp.stack(new_sc), jnp.stack(new_pool))

    bp, dt = x_prompt.shape[0], x_prompt.dtype
    y_p, k_p, v_p, conv_p, sc_p, pool_p = trunk(
        x_prompt, 0,
        lambda i: jnp.zeros((bp, CONV_WIDTH - 1, CONV_DIM), dt),
        lambda i: jnp.zeros((bp, 0, SB_HEADS, SB_HEAD_DIM), dt),
        lambda i: jnp.zeros((bp, 0, SB_HEADS, SB_HEAD_DIM), dt),
        lambda i: jnp.zeros((bp, SC_WIDTH - 1, SC_DIM), dt),
        lambda i: jnp.zeros((bp, POOL_MAX - 1, POOL_DIM), dt))

    bs = x_sample.shape[0]
    past_len = page_table.shape[1] * cache_k.shape[2]
    gather = lambda cache, i: cache[i][page_table].reshape(bs, past_len, SB_HEADS, SB_HEAD_DIM)
    y_s, k_s, v_s, conv_s, sc_s, pool_s = trunk(
        x_sample, past_len,
        lambda i: state_conformer[i],
        lambda i: gather(cache_k, i),
        lambda i: gather(cache_v, i),
        lambda i: state_shortconv[i],
        lambda i: state_pool[i])

    return (y_p, y_s, k_p, v_p, k_s, v_s, conv_p, conv_s, sc_p, sc_s, pool_p, pool_s)
```

```python
import functools

import jax
import jax.numpy as jnp
from jax import lax
from jax.experimental import pallas as pl
from jax.experimental.pallas import tpu as pltpu

F32 = jnp.float32
BF16 = jnp.bfloat16

EPS = 1e-6
CONV_WIDTH = 31
SC_WIDTH = 3
POOL_WINDOWS = (2, 4, 8, 16)
POOL_MAX = max(POOL_WINDOWS)
N_HEADS = 8
HEAD_DIM = 64

LANES = 128
SUBLANES = 8
ATTN_BLOCK = 256
VMEM_LIMIT = 56 * 1024 * 1024


def _round_up(n, m):
    return -(-n // m) * m


def _rms_norm(x, g):
    ms = jnp.mean(x * x, axis=-1, keepdims=True)
    return x * lax.rsqrt(ms + EPS) * g


def _softplus(z):
    return jnp.maximum(z, 0.0) + jnp.log(1.0 + jnp.exp(-jnp.abs(z)))


def _split_bf16(x):
    hi = x.astype(BF16)
    lo = (x - hi.astype(F32)).astype(BF16)
    return hi, lo


def _dot(a, b):
    return jnp.dot(a, b, preferred_element_type=F32)


def _dot_nt(a, b):
    return lax.dot_general(a, b, (((1,), (1,)), ((), ())), preferred_element_type=F32)


def _dot_split(x, m):
    hi, lo = _split_bf16(x)
    return _dot(hi, m) + _dot(lo, m)


def _params(*sem):
    return pltpu.CompilerParams(dimension_semantics=sem, vmem_limit_bytes=VMEM_LIMIT)


def _const_spec(shape):
    nd = len(shape)
    return pl.BlockSpec(shape, lambda *_: (0,) * nd)


def _even_proj_kernel(x_ref, g_ref, w_ref, glu_ref, q_ref, k_ref, v_ref, kb_ref, vb_ref, *, cdim, scale):
    hn = _rms_norm(x_ref[...], g_ref[...]).astype(BF16)

    def proj(c):
        return _dot(hn, w_ref[:, c * cdim:(c + 1) * cdim])

    glu_ref[...] = proj(0) * jax.nn.sigmoid(proj(1))
    q_ref[...] = proj(2) * scale
    k = proj(3)
    k_ref[...] = k
    kb_ref[...] = k.astype(BF16)
    v = proj(4)
    v_ref[...] = v
    vb_ref[...] = v.astype(BF16)


def _even_proj(x, g, w, *, tm):
    t, d = x.shape
    cdim = w.shape[1] // 5
    row = lambda dt: jax.ShapeDtypeStruct((t, cdim), dt)
    blk = pl.BlockSpec((tm, cdim), lambda i: (i, 0))
    return pl.pallas_call(
        functools.partial(_even_proj_kernel, cdim=cdim, scale=HEAD_DIM ** -0.5),
        grid=(t // tm,),
        in_specs=[pl.BlockSpec((tm, d), lambda i: (i, 0)), _const_spec((1, d)), _const_spec(w.shape)],
        out_specs=[blk] * 6,
        out_shape=[row(F32), row(F32), row(F32), row(F32), row(BF16), row(BF16)],
        compiler_params=_params("parallel"),
        name="even_proj",
    )(x, g, w)


def _odd_proj_kernel(x_ref, g_ref, w_ref, gb_ref, gcx_ref, u_ref, *, cdim):
    hn = _rms_norm(x_ref[...], g_ref[...]).astype(BF16)

    def proj(c):
        return _dot(hn, w_ref[:, c * cdim:(c + 1) * cdim])

    gb_ref[...] = proj(0)
    gcx_ref[...] = proj(1) * proj(2)
    u_ref[...] = proj(3)


def _odd_proj(x, g, w, *, tm):
    t, d = x.shape
    cdim = w.shape[1] // 4
    blk = pl.BlockSpec((tm, cdim), lambda i: (i, 0))
    return pl.pallas_call(
        functools.partial(_odd_proj_kernel, cdim=cdim),
        grid=(t // tm,),
        in_specs=[pl.BlockSpec((tm, d), lambda i: (i, 0)), _const_spec((1, d)), _const_spec(w.shape)],
        out_specs=[blk] * 3,
        out_shape=[jax.ShapeDtypeStruct((t, cdim), F32)] * 3,
        compiler_params=_params("parallel"),
        name="odd_proj",
    )(x, g, w)


def _out_ffn_kernel(h_ref, a_ref, b_ref, wo_ref, g_ref, w1_ref, w2_ref, gf_ref, o_ref, act_ref,
                    *, ffn, chunk, final_norm):
    ca = a_ref.shape[1]
    mix = _dot(a_ref[...].astype(BF16), wo_ref[:ca, :]) + _dot(b_ref[...].astype(BF16), wo_ref[ca:, :])
    h1 = h_ref[...] + mix
    hn = _rms_norm(h1, g_ref[...]).astype(BF16)
    for c in range(ffn // chunk):
        lo = c * chunk
        gate = _dot(hn, w1_ref[:, lo:lo + chunk])
        up = _dot(hn, w1_ref[:, ffn + lo:ffn + lo + chunk])
        act_ref[:, lo:lo + chunk] = (gate * jax.nn.sigmoid(gate) * up).astype(BF16)
    h2 = h1 + _dot(act_ref[...], w2_ref[...])
    if final_norm:
        h2 = _rms_norm(h2, gf_ref[...])
    o_ref[...] = h2


def _out_ffn(h, a, b, wo, g, w1, w2, gf, *, tm, final_norm):
    t, d = h.shape
    ffn = w2.shape[0]
    chunk = 2 * LANES
    assert ffn % chunk == 0
    single = pl.Buffered(1)
    wspec = lambda shape: pl.BlockSpec(shape, lambda i: (0, 0), pipeline_mode=single)
    return pl.pallas_call(
        functools.partial(_out_ffn_kernel, ffn=ffn, chunk=chunk, final_norm=final_norm),
        grid=(t // tm,),
        in_specs=[
            pl.BlockSpec((tm, d), lambda i: (i, 0)),
            pl.BlockSpec((tm, a.shape[1]), lambda i: (i, 0)),
            pl.BlockSpec((tm, b.shape[1]), lambda i: (i, 0)),
            wspec(wo.shape), _const_spec((1, d)), wspec(w1.shape), wspec(w2.shape), _const_spec((1, d)),
        ],
        out_specs=pl.BlockSpec((tm, d), lambda i: (i, 0)),
        out_shape=jax.ShapeDtypeStruct((t, d), F32),
        scratch_shapes=[pltpu.VMEM((tm, ffn), BF16)],
        compiler_params=_params("parallel"),
        name="out_ffn",
    )(h, a, b, wo, g, w1, w2, gf)


def _load_history(ext_ref, prev_ref, x_ref, *, hist, tl, first):
    hp = _round_up(hist, SUBLANES)

    @pl.when(first)
    def _():
        ext_ref[:, hp - hist:hp, :] = prev_ref[...]

    @pl.when(jnp.logical_not(first))
    def _():
        ext_ref[:, hp - hist:hp, :] = ext_ref[:, hp - hist + tl:hp + tl, :]

    ext_ref[:, hp:hp + tl, :] = x_ref[...]


def _conformer_kernel(prev_ref, x_ref, cw_ref, cb_ref, lg_ref, lb_ref, a_ref, st_ref, ext_ref,
                      *, nb, tl, nbc, rows):
    li = pl.program_id(1)
    hist = CONV_WIDTH - 1
    hp = _round_up(hist, SUBLANES)
    _load_history(ext_ref, prev_ref, x_ref, hist=hist, tl=tl, first=li == 0)
    for b0 in range(0, nb, nbc):
        for r0 in range(0, tl, rows):
            acc = None
            for w in range(CONV_WIDTH):
                s = hp - hist + r0 + w
                term = ext_ref[b0:b0 + nbc, s:s + rows, :] * cw_ref[w:w + 1, :]
                acc = term if acc is None else acc + term
            y = acc + cb_ref[...]
            mu = jnp.mean(y, axis=-1, keepdims=True)
            yc = y - mu
            var = jnp.mean(yc * yc, axis=-1, keepdims=True)
            y = yc * lax.rsqrt(var + EPS) * lg_ref[...] + lb_ref[...]
            a_ref[b0:b0 + nbc, r0:r0 + rows, :] = y * jax.nn.sigmoid(y)

    @pl.when(li == pl.num_programs(1) - 1)
    def _():
        st_ref[...] = ext_ref[:, hp + tl - hist:hp + tl, :]


def _conformer(prev, x, cw, cb, lg, lb, *, nb, tl, nbc, rows):
    bsz, seq, c = x.shape
    hist = CONV_WIDTH - 1
    assert seq == tl or tl >= hist
    blk = pl.BlockSpec((nb, tl, c), lambda i, j: (i, j, 0))
    st = pl.BlockSpec((nb, hist, c), lambda i, j: (i, 0, 0))
    return pl.pallas_call(
        functools.partial(_conformer_kernel, nb=nb, tl=tl, nbc=nbc, rows=rows),
        grid=(bsz // nb, seq // tl),
        in_specs=[st, blk, _const_spec(cw.shape), _const_spec((1, c)), _const_spec((1, c)), _const_spec((1, c))],
        out_specs=[blk, st],
        out_shape=[jax.ShapeDtypeStruct((bsz, seq, c), F32), jax.ShapeDtypeStruct((bsz, hist, c), F32)],
        scratch_shapes=[pltpu.VMEM((nb, _round_up(hist, SUBLANES) + tl, c), F32)],
        compiler_params=_params("parallel", "arbitrary"),
        name="conformer_conv",
    )(prev, x, cw, cb, lg, lb)


def _odd_mixer_kernel(scp_ref, plp_ref, gb_ref, gcx_ref, u_ref, cw_ref, pw_ref, ps_ref,
                      c_ref, d_ref, nsc_ref, npl_ref, esc_ref, epl_ref, pooled_ref,
                      *, start, nb, tl, nbc, rows):
    li = pl.program_id(1)
    first = li == 0
    sch, plh = SC_WIDTH - 1, POOL_MAX - 1
    scp, plp = _round_up(sch, SUBLANES), _round_up(plh, SUBLANES)
    _load_history(esc_ref, scp_ref, gcx_ref, hist=sch, tl=tl, first=first)
    _load_history(epl_ref, plp_ref, u_ref, hist=plh, tl=tl, first=first)
    group = c_ref.shape[2] // len(POOL_WINDOWS)
    for b0 in range(0, nb, nbc):
        bs = slice(b0, b0 + nbc)
        for r0 in range(0, tl, rows):
            conv = None
            for w in range(SC_WIDTH):
                s = scp - sch + r0 + w
                term = esc_ref[bs, s:s + rows, :] * cw_ref[w:w + 1, :]
                conv = term if conv is None else conv + term
            c_ref[bs, r0:r0 + rows, :] = gb_ref[bs, r0:r0 + rows, :] * conv
            pos = start + li * tl + r0 + lax.broadcasted_iota(jnp.int32, (nbc, rows, group), 1)
            for gi, win in enumerate(POOL_WINDOWS):
                ls = slice(gi * group, (gi + 1) * group)
                wsum = None
                for i in range(win):
                    s = plp + r0 - i
                    term = epl_ref[bs, s:s + rows, ls]
                    wsum = term if wsum is None else wsum + term
                cnt = jnp.minimum(win, pos + 1).astype(F32)
                pooled_ref[bs, r0:r0 + rows, ls] = wsum / cnt - u_ref[bs, r0:r0 + rows, ls]
    pooled = pooled_ref[...].reshape(nb * tl, c_ref.shape[2]).astype(BF16)
    for gi in range(len(POOL_WINDOWS)):
        ls = slice(gi * group, (gi + 1) * group)
        dg = _dot(pooled[:, ls], pw_ref[gi]) * ps_ref[:, ls]
        d_ref[:, :, ls] = dg.reshape(nb, tl, group)

    @pl.when(li == pl.num_programs(1) - 1)
    def _():
        nsc_ref[...] = esc_ref[:, scp + tl - sch:scp + tl, :]
        npl_ref[...] = epl_ref[:, plp + tl - plh:plp + tl, :]


def _odd_mixer(sc_prev, pool_prev, gb, gcx, u, cw, pw, ps, *, start, nb, tl, nbc, rows):
    bsz, seq, c = gb.shape
    sch, plh = SC_WIDTH - 1, POOL_MAX - 1
    assert seq == tl or tl >= plh
    blk = pl.BlockSpec((nb, tl, c), lambda i, j: (i, j, 0))
    st = lambda h: pl.BlockSpec((nb, h, c), lambda i, j: (i, 0, 0))
    return pl.pallas_call(
        functools.partial(_odd_mixer_kernel, start=start, nb=nb, tl=tl, nbc=nbc, rows=rows),
        grid=(bsz // nb, seq // tl),
        in_specs=[st(sch), st(plh), blk, blk, blk, _const_spec(cw.shape), _const_spec(pw.shape),
                  _const_spec((1, c))],
        out_specs=[blk, blk, st(sch), st(plh)],
        out_shape=[jax.ShapeDtypeStruct((bsz, seq, c), F32), jax.ShapeDtypeStruct((bsz, seq, c), F32),
                   jax.ShapeDtypeStruct((bsz, sch, c), F32), jax.ShapeDtypeStruct((bsz, plh, c), F32)],
        scratch_shapes=[pltpu.VMEM((nb, _round_up(sch, SUBLANES) + tl, c), F32),
                        pltpu.VMEM((nb, _round_up(plh, SUBLANES) + tl, c), F32),
                        pltpu.VMEM((nb, tl, c), F32)],
        compiler_params=_params("parallel", "arbitrary"),
        name="shortconv_pool",
    )(sc_prev, pool_prev, gb, gcx, u, cw, pw, ps)


def _suffix_matrix(n):
    return (lax.broadcasted_iota(jnp.int32, (n, n), 0) >= lax.broadcasted_iota(jnp.int32, (n, n), 1)).astype(BF16)


def _prompt_attn_kernel(bias_ref, q_ref, k_ref, v_ref, o_ref, *, blk):
    pair = pl.program_id(1)
    qi = pl.program_id(2)
    lane = lax.broadcasted_iota(jnp.int32, (blk, LANES), 1)
    causal = lax.broadcasted_iota(jnp.int32, (blk, blk), 1) < lax.broadcasted_iota(jnp.int32, (blk, blk), 0)
    suffix = _suffix_matrix(blk)
    ones = jnp.ones((blk, LANES), BF16)
    q = q_ref[...]
    accs = []
    for a in range(LANES // HEAD_DIM):
        in_head = (lane >= a * HEAD_DIM) & (lane < (a + 1) * HEAD_DIM)
        qm = jnp.where(in_head, q, 0.0).astype(BF16)
        bias = bias_ref[pair * (LANES // HEAD_DIM) + a]

        def block(j, carry, acc, masked, qm=qm, bias=bias):
            rows = pl.ds(pl.multiple_of(j * blk, blk), blk)
            z = _dot_nt(qm, k_ref[rows, :]) + bias
            sp = _softplus(z)
            if masked:
                sp = jnp.where(causal, sp, 0.0)
            hi, lo = _split_bf16(sp)
            within = _dot(hi, suffix) + _dot(lo, suffix)
            total = _dot(hi, ones) + _dot(lo, ones)
            w = jnp.exp(z - within - jnp.concatenate([carry] * (blk // LANES), axis=1))
            if masked:
                w = jnp.where(causal, w, 0.0)
            acc = acc + _dot(w.astype(BF16), v_ref[rows, :])
            return carry + total, acc

        zero = jnp.zeros((blk, LANES), F32)
        carry, acc = block(qi, zero, zero, True)
        carry, acc = lax.fori_loop(0, qi, lambda t, c: block(qi - 1 - t, c[0], c[1], False), (carry, acc))
        accs.append(acc)
    out = accs[0]
    for a in range(1, len(accs)):
        out = jnp.where(lane >= a * HEAD_DIM, accs[a], out)
    o_ref[...] = out


def _prompt_attn(bias, q, kb, vb):
    bsz, seq, c = q.shape
    blk = ATTN_BLOCK
    assert seq % blk == 0 and c % LANES == 0
    qspec = pl.BlockSpec((None, blk, LANES), lambda b, p, i: (b, i, p))
    kvspec = pl.BlockSpec((None, seq, LANES), lambda b, p, i: (b, 0, p))
    return pl.pallas_call(
        functools.partial(_prompt_attn_kernel, blk=blk),
        grid=(bsz, c // LANES, seq // blk),
        in_specs=[pl.BlockSpec(memory_space=pltpu.SMEM), qspec, kvspec, kvspec],
        out_specs=qspec,
        out_shape=jax.ShapeDtypeStruct((bsz, seq, c), F32),
        compiler_params=_params("parallel", "parallel", "arbitrary"),
        name="prompt_attention",
    )(bias, q, kb, vb)


def _sample_attn_kernel(pt_ref, bias_ref, q_ref, kn_ref, vn_ref, *refs, n_pages, page, kblk):
    del pt_ref
    k_pages, v_pages, o_ref = refs[:n_pages], refs[n_pages:2 * n_pages], refs[2 * n_pages]
    lq, c = q_ref.shape
    rows = N_HEADS * lq
    ppb = kblk // page
    n_blk = n_pages // ppb

    row_head = lax.broadcasted_iota(jnp.int32, (rows, c), 0) // lq
    lane_head = lax.broadcasted_iota(jnp.int32, (rows, c), 1) // HEAD_DIM
    qh = jnp.where(row_head == lane_head, jnp.concatenate([q_ref[...]] * N_HEADS, axis=0), 0.0).astype(BF16)
    bias = bias_ref[...]

    pad = jnp.zeros((LANES - lq, c), F32)
    kn = jnp.concatenate([kn_ref[...], pad], axis=0).astype(BF16)
    vn = jnp.concatenate([vn_ref[...], pad], axis=0).astype(BF16)
    q_idx = lax.broadcasted_iota(jnp.int32, (rows, LANES), 0) % lq
    valid = lax.broadcasted_iota(jnp.int32, (rows, LANES), 1) < q_idx
    zn = _dot_nt(qh, kn) + bias[:, :LANES]
    spn = jnp.where(valid, _softplus(zn), 0.0)
    wn = jnp.where(valid, jnp.exp(zn - _dot_split(spn, _suffix_matrix(LANES))), 0.0)
    out = _dot(wn.astype(BF16), vn)
    carry = _dot_split(spn, jnp.ones((LANES, LANES), BF16))

    suffix = _suffix_matrix(kblk)
    ones = jnp.ones((kblk, LANES), BF16)
    for j in reversed(range(n_blk)):
        kj = jnp.concatenate([k_pages[j * ppb + i][...] for i in range(ppb)], axis=1).astype(BF16)
        vj = jnp.concatenate([v_pages[j * ppb + i][...] for i in range(ppb)], axis=1).astype(BF16)
        z = _dot(qh, kj) + bias
        hi, lo = _split_bf16(_softplus(z))
        within = _dot(hi, suffix) + _dot(lo, suffix)
        w = jnp.exp(z - within - jnp.concatenate([carry] * (kblk // LANES), axis=1))
        out = out + _dot_nt(w.astype(BF16), vj)
        carry = carry + _dot(hi, ones) + _dot(lo, ones)

    lane_head = lax.broadcasted_iota(jnp.int32, (lq, c), 1) // HEAD_DIM
    res = jnp.zeros((lq, c), F32)
    for h in range(N_HEADS):
        res = jnp.where(lane_head == h, out[h * lq:(h + 1) * lq, :], res)
    o_ref[...] = res


def _sample_attn(page_table, bias, q, kn, vn, cache_k, cache_v, page_base):
    bsz, lq, c = q.shape
    n_pages = page_table.shape[1]
    page = cache_k.shape[2]
    kblk = 2 * LANES
    rows = N_HEADS * lq
    bias_rows = jnp.broadcast_to(jnp.repeat(bias, lq)[:, None], (rows, kblk))
    new = pl.BlockSpec((None, lq, c), lambda b, pt: (b, 0, 0))
    page_spec = lambda p: pl.BlockSpec((None, c, page), lambda b, pt: (page_base + pt[b, p], 0, 0))
    grid_spec = pltpu.PrefetchScalarGridSpec(
        num_scalar_prefetch=1,
        grid=(bsz,),
        in_specs=[pl.BlockSpec((rows, kblk), lambda b, pt: (0, 0)), new, new, new]
        + [page_spec(p) for p in range(n_pages)] * 2,
        out_specs=new,
    )
    return pl.pallas_call(
        functools.partial(_sample_attn_kernel, n_pages=n_pages, page=page, kblk=kblk),
        grid_spec=grid_spec,
        out_shape=jax.ShapeDtypeStruct((bsz, lq, c), F32),
        compiler_params=_params("parallel"),
        name="sample_attention",
    )(page_table, bias_rows, q, kn, vn, *([cache_k] * n_pages), *([cache_v] * n_pages))


def _trunk(x, start, states, attn_fn, w, *, tm, seq_tiles):
    bsz, seq, d = x.shape
    t = bsz * seq
    h = x.reshape(t, d)
    depth = w["norm_mix_g"].shape[0]
    row = lambda v: v.reshape(1, -1)
    new_conv, new_k, new_v, new_sc, new_pool = [], [], [], [], []
    for layer in range(depth):
        i = layer // 2
        g_mix = row(w["norm_mix_g"][layer])
        if layer % 2 == 0:
            glu, q, k, v, kb, vb = _even_proj(h, g_mix, w["w_in_even"][i], tm=tm)
            c = glu.shape[1]
            seq3 = lambda z: z.reshape(bsz, seq, c)
            a, conv_state = _conformer(states["conv"](i), seq3(glu), w["conv_a_w"][i], row(w["conv_a_b"][i]),
                                       row(w["ln_a_g"][i]), row(w["ln_a_b"][i]), **seq_tiles)
            b = attn_fn(i, seq3(q), seq3(k), seq3(v), seq3(kb), seq3(vb))
            new_conv.append(conv_state)
            new_k.append(k.reshape(bsz, seq, N_HEADS, HEAD_DIM))
            new_v.append(v.reshape(bsz, seq, N_HEADS, HEAD_DIM))
            m1, m2, w_out = a.reshape(t, c), b.reshape(t, c), w["w_out_even"][i]
        else:
            gb, gcx, u = _odd_proj(h, g_mix, w["w_in_odd"][i], tm=tm)
            c = gb.shape[1]
            seq3 = lambda z: z.reshape(bsz, seq, c)
            cc, dd, sc_state, pool_state = _odd_mixer(
                states["sc"](i), states["pool"](i), seq3(gb), seq3(gcx), seq3(u), w["conv_c_w"][i],
                w["pool_w"][i], row(w["pool_scale"][i]), start=start, **seq_tiles)
            new_sc.append(sc_state)
            new_pool.append(pool_state)
            m1, m2, w_out = cc.reshape(t, c), dd.reshape(t, c), w["w_out_odd"][i]
        h = _out_ffn(h, m1, m2, w_out, row(w["norm_ffn_g"][layer]), w["w_ffn_in"][layer], w["w_ffn_out"][layer],
                     row(w["norm_final_g"]), tm=tm, final_norm=layer == depth - 1)
    return (h.reshape(bsz, seq, d), jnp.stack(new_k), jnp.stack(new_v), jnp.stack(new_conv),
            jnp.stack(new_sc), jnp.stack(new_pool))


def kernel(x_prompt, x_sample, cache_k, cache_v, state_conformer, state_shortconv, state_pool, page_table,
           norm_mix_g, norm_ffn_g, norm_final_g, w_in_even, conv_a_w, conv_a_b, ln_a_g, ln_a_b, sb_bias,
           w_out_even, w_in_odd, conv_c_w, pool_w, pool_scale, w_out_odd, w_ffn_in, w_ffn_out):
    w = dict(norm_mix_g=norm_mix_g, norm_ffn_g=norm_ffn_g, norm_final_g=norm_final_g,
             w_in_even=w_in_even.astype(BF16), conv_a_w=conv_a_w, conv_a_b=conv_a_b, ln_a_g=ln_a_g, ln_a_b=ln_a_b,
             w_out_even=w_out_even.astype(BF16), w_in_odd=w_in_odd.astype(BF16), conv_c_w=conv_c_w,
             pool_w=pool_w.astype(BF16), pool_scale=pool_scale, w_out_odd=w_out_odd.astype(BF16),
             w_ffn_in=w_ffn_in.astype(BF16), w_ffn_out=w_ffn_out.astype(BF16))
    bp, dt = x_prompt.shape[0], x_prompt.dtype
    conv_dim, sc_dim, pool_dim = state_conformer.shape[-1], state_shortconv.shape[-1], state_pool.shape[-1]

    prompt_states = dict(conv=lambda i: jnp.zeros((bp, CONV_WIDTH - 1, conv_dim), dt),
                         sc=lambda i: jnp.zeros((bp, SC_WIDTH - 1, sc_dim), dt),
                         pool=lambda i: jnp.zeros((bp, POOL_MAX - 1, pool_dim), dt))
    prompt_attn = lambda i, q, k, v, kb, vb: _prompt_attn(sb_bias[i], q, kb, vb)
    y_p, k_p, v_p, conv_p, sc_p, pool_p = _trunk(
        x_prompt, 0, prompt_states, prompt_attn, w, tm=512, seq_tiles=dict(nb=1, tl=256, nbc=1, rows=32))

    n_pool, page = cache_k.shape[1], cache_k.shape[2]
    past_len = page_table.shape[1] * page
    pages = lambda cache: jnp.transpose(cache, (0, 1, 3, 4, 2)).reshape(
        cache.shape[0] * n_pool, N_HEADS * HEAD_DIM, page)
    sample_states = dict(conv=lambda i: state_conformer[i], sc=lambda i: state_shortconv[i],
                         pool=lambda i: state_pool[i])
    sample_attn = lambda i, q, k, v, kb, vb: _sample_attn(
        page_table, sb_bias[i], q, k, v, pages(cache_k), pages(cache_v), i * n_pool)
    y_s, k_s, v_s, conv_s, sc_s, pool_s = _trunk(
        x_sample, past_len, sample_states, sample_attn, w, tm=512,
        seq_tiles=dict(nb=16, tl=x_sample.shape[1], nbc=4, rows=x_sample.shape[1]))

    return (y_p, y_s, k_p, v_p, k_s, v_s, conv_p, conv_s, sc_p, sc_s, pool_p, pool_s)
```

```python
import functools
import math

import jax
import jax.numpy as jnp
from jax import lax
from jax.experimental import pallas as pl
from jax.experimental.pallas import tpu as pltpu

F32 = jnp.float32
BF16 = jnp.bfloat16

EPS = 1e-6
CONV_WIDTH = 31
SC_WIDTH = 3
POOL_WINDOWS = (2, 4, 8, 16)
POOL_MAX = max(POOL_WINDOWS)
N_HEADS = 8
HEAD_DIM = 64
LOG2E = math.log2(math.e)

LANES = 128
SUBLANES = 8
ATTN_BLOCK = 256
VMEM_LIMIT = 56 * 1024 * 1024


def _round_up(n, m):
    return -(-n // m) * m


def _rms_norm(x, g):
    ms = jnp.mean(x * x, axis=-1, keepdims=True)
    return x * lax.rsqrt(ms + EPS) * g


def _softplus2(z):
    return jnp.maximum(z, 0.0) + jnp.log2(1.0 + jnp.exp2(-jnp.abs(z)))


def _hi_lo(x):
    hi = x.astype(BF16)
    lo = (x - hi.astype(F32)).astype(BF16)
    return jnp.concatenate([hi, lo], axis=1)


def _dot(a, b):
    return jnp.dot(a, b, preferred_element_type=F32)


def _dot_nt(a, b):
    return lax.dot_general(a, b, (((1,), (1,)), ((), ())), preferred_element_type=F32)


def _params(*sem):
    return pltpu.CompilerParams(dimension_semantics=sem, vmem_limit_bytes=VMEM_LIMIT)


def _const_spec(shape):
    nd = len(shape)
    return pl.BlockSpec(shape, lambda *_: (0,) * nd)


def _even_proj_kernel(x_ref, g_ref, w_ref, wt_ref, glu_ref, q_ref, *kv_refs, cdim, qscale, transposed_kv):
    hn = _rms_norm(x_ref[...], g_ref[...]).astype(BF16)

    def proj(c):
        return _dot(hn, w_ref[:, c * cdim:(c + 1) * cdim])

    glu_ref[...] = proj(0) * jax.nn.sigmoid(proj(1))
    q_ref[...] = proj(2) * qscale
    if transposed_kv:
        kt_ref, vt_ref, ktb_ref, vtb_ref = kv_refs
        for t_ref, tb_ref, c in ((kt_ref, ktb_ref, 0), (vt_ref, vtb_ref, 1)):
            xt = _dot_nt(wt_ref[c * cdim:(c + 1) * cdim, :], hn)
            t_ref[...] = xt
            tb_ref[...] = xt.astype(BF16)
    else:
        k_ref, v_ref = kv_refs
        k_ref[...] = proj(3)
        v_ref[...] = proj(4)


def _even_proj(x, g, w, wt, *, bsz, tm, transposed_kv):
    t, d = x.shape
    cdim = w.shape[1] // 5
    seq = t // bsz
    row = jax.ShapeDtypeStruct((t, cdim), F32)
    blk = pl.BlockSpec((tm, cdim), lambda i: (i, 0))
    if transposed_kv:
        assert seq % tm == 0
        nl = seq // tm
        tr = lambda dt: jax.ShapeDtypeStruct((bsz, cdim, seq), dt)
        tblk = pl.BlockSpec((None, cdim, tm), lambda i: (i // nl, 0, i % nl))
        out_shape, out_specs = [row, row, tr(F32), tr(F32), tr(BF16), tr(BF16)], [blk, blk] + [tblk] * 4
    else:
        out_shape, out_specs = [row] * 4, [blk] * 4
    return pl.pallas_call(
        functools.partial(_even_proj_kernel, cdim=cdim, qscale=HEAD_DIM ** -0.5 * LOG2E,
                          transposed_kv=transposed_kv),
        grid=(t // tm,),
        in_specs=[pl.BlockSpec((tm, d), lambda i: (i, 0)), _const_spec((1, d)), _const_spec(w.shape),
                  _const_spec(wt.shape)],
        out_specs=out_specs,
        out_shape=out_shape,
        compiler_params=_params("parallel"),
        name="even_proj",
    )(x, g, w, wt)


def _odd_proj_kernel(x_ref, g_ref, w_ref, gb_ref, gcx_ref, u_ref, *, cdim):
    hn = _rms_norm(x_ref[...], g_ref[...]).astype(BF16)

    def proj(c):
        return _dot(hn, w_ref[:, c * cdim:(c + 1) * cdim])

    gb_ref[...] = proj(0)
    gcx_ref[...] = proj(1) * proj(2)
    u_ref[...] = proj(3)


def _odd_proj(x, g, w, *, tm):
    t, d = x.shape
    cdim = w.shape[1] // 4
    blk = pl.BlockSpec((tm, cdim), lambda i: (i, 0))
    return pl.pallas_call(
        functools.partial(_odd_proj_kernel, cdim=cdim),
        grid=(t // tm,),
        in_specs=[pl.BlockSpec((tm, d), lambda i: (i, 0)), _const_spec((1, d)), _const_spec(w.shape)],
        out_specs=[blk] * 3,
        out_shape=[jax.ShapeDtypeStruct((t, cdim), F32)] * 3,
        compiler_params=_params("parallel"),
        name="odd_proj",
    )(x, g, w)


def _out_ffn_kernel(h_ref, a_ref, b_ref, wo_ref, g_ref, w1_ref, w2_ref, gf_ref, o_ref, act_ref,
                    *, ffn, chunk, final_norm):
    ca = a_ref.shape[1]
    mix = _dot(a_ref[...].astype(BF16), wo_ref[:ca, :]) + _dot(b_ref[...].astype(BF16), wo_ref[ca:, :])
    h1 = h_ref[...] + mix
    hn = _rms_norm(h1, g_ref[...]).astype(BF16)
    for c in range(ffn // chunk):
        lo = c * chunk
        gate = _dot(hn, w1_ref[:, lo:lo + chunk])
        up = _dot(hn, w1_ref[:, ffn + lo:ffn + lo + chunk])
        act_ref[:, lo:lo + chunk] = (gate * jax.nn.sigmoid(gate) * up).astype(BF16)
    h2 = h1 + _dot(act_ref[...], w2_ref[...])
    if final_norm:
        h2 = _rms_norm(h2, gf_ref[...])
    o_ref[...] = h2


def _out_ffn(h, a, b, wo, g, w1, w2, gf, *, tm, final_norm):
    t, d = h.shape
    ffn = w2.shape[0]
    chunk = 2 * LANES
    assert ffn % chunk == 0
    single = pl.Buffered(1)
    wspec = lambda shape: pl.BlockSpec(shape, lambda i: (0, 0), pipeline_mode=single)
    return pl.pallas_call(
        functools.partial(_out_ffn_kernel, ffn=ffn, chunk=chunk, final_norm=final_norm),
        grid=(t // tm,),
        in_specs=[
            pl.BlockSpec((tm, d), lambda i: (i, 0)),
            pl.BlockSpec((tm, a.shape[1]), lambda i: (i, 0)),
            pl.BlockSpec((tm, b.shape[1]), lambda i: (i, 0)),
            wspec(wo.shape), _const_spec((1, d)), wspec(w1.shape), wspec(w2.shape), _const_spec((1, d)),
        ],
        out_specs=pl.BlockSpec((tm, d), lambda i: (i, 0)),
        out_shape=jax.ShapeDtypeStruct((t, d), F32),
        scratch_shapes=[pltpu.VMEM((tm, ffn), BF16)],
        compiler_params=_params("parallel"),
        name="out_ffn",
    )(h, a, b, wo, g, w1, w2, gf)


def _load_history(ext_ref, prev_ref, x_ref, *, hist, tl, first):
    hp = _round_up(hist, SUBLANES)

    @pl.when(first)
    def _():
        ext_ref[:, hp - hist:hp, :] = prev_ref[...]

    @pl.when(jnp.logical_not(first))
    def _():
        ext_ref[:, hp - hist:hp, :] = ext_ref[:, hp - hist + tl:hp + tl, :]

    ext_ref[:, hp:hp + tl, :] = x_ref[...]


def _conformer_kernel(prev_ref, x_ref, cw_ref, cb_ref, lg_ref, lb_ref, a_ref, st_ref, ext_ref, sh_ref,
                      *, nb, tl, nbc, rows):
    li = pl.program_id(1)
    hist = CONV_WIDTH - 1
    hp = _round_up(hist, SUBLANES)
    _load_history(ext_ref, prev_ref, x_ref, hist=hist, tl=tl, first=li == 0)
    n_sh = hp + tl - SUBLANES
    piece = 8 * SUBLANES
    for r in range(1, SUBLANES):
        for b0 in range(0, nb, nbc):
            for c0 in range(0, n_sh, piece):
                n = min(piece, n_sh - c0)
                sh_ref[r - 1, b0:b0 + nbc, c0:c0 + n, :] = ext_ref[b0:b0 + nbc, c0 + r:c0 + r + n, :]
    for b0 in range(0, nb, nbc):
        bs = slice(b0, b0 + nbc)
        for r0 in range(0, tl, rows):
            acc = None
            for w in range(CONV_WIDTH):
                s = hp - hist + w
                base, r = s - s % SUBLANES + r0, s % SUBLANES
                tap = ext_ref[bs, base:base + rows, :] if r == 0 else sh_ref[r - 1, bs, base:base + rows, :]
                term = tap * cw_ref[w:w + 1, :]
                acc = term if acc is None else acc + term
            y = acc + cb_ref[...]
            mu = jnp.mean(y, axis=-1, keepdims=True)
            yc = y - mu
            var = jnp.mean(yc * yc, axis=-1, keepdims=True)
            y = yc * lax.rsqrt(var + EPS) * lg_ref[...] + lb_ref[...]
            a_ref[bs, r0:r0 + rows, :] = y * jax.nn.sigmoid(y)

    @pl.when(li == pl.num_programs(1) - 1)
    def _():
        st_ref[...] = ext_ref[:, hp + tl - hist:hp + tl, :]


def _conformer(prev, x, cw, cb, lg, lb, *, nb, tl, nbc, rows):
    bsz, seq, c = x.shape
    hist = CONV_WIDTH - 1
    hp = _round_up(hist, SUBLANES)
    assert seq == tl or tl >= hist
    blk = pl.BlockSpec((nb, tl, c), lambda i, j: (i, j, 0))
    st = pl.BlockSpec((nb, hist, c), lambda i, j: (i, 0, 0))
    return pl.pallas_call(
        functools.partial(_conformer_kernel, nb=nb, tl=tl, nbc=nbc, rows=rows),
        grid=(bsz // nb, seq // tl),
        in_specs=[st, blk, _const_spec(cw.shape), _const_spec((1, c)), _const_spec((1, c)), _const_spec((1, c))],
        out_specs=[blk, st],
        out_shape=[jax.ShapeDtypeStruct((bsz, seq, c), F32), jax.ShapeDtypeStruct((bsz, hist, c), F32)],
        scratch_shapes=[pltpu.VMEM((nb, hp + tl, c), F32),
                        pltpu.VMEM((SUBLANES - 1, nb, hp + tl - SUBLANES, c), F32)],
        compiler_params=_params("parallel", "arbitrary"),
        name="conformer_conv",
    )(prev, x, cw, cb, lg, lb)


def _odd_mixer_kernel(scp_ref, plp_ref, gb_ref, gcx_ref, u_ref, cw_ref, pw_ref, ps_ref,
                      c_ref, d_ref, nsc_ref, npl_ref, esc_ref, epl_ref, pooled_ref,
                      *, start, nb, tl, nbc, rows):
    li = pl.program_id(1)
    first = li == 0
    sch, plh = SC_WIDTH - 1, POOL_MAX - 1
    scp, plp = _round_up(sch, SUBLANES), _round_up(plh, SUBLANES)
    _load_history(esc_ref, scp_ref, gcx_ref, hist=sch, tl=tl, first=first)
    _load_history(epl_ref, plp_ref, u_ref, hist=plh, tl=tl, first=first)
    group = c_ref.shape[2] // len(POOL_WINDOWS)
    for b0 in range(0, nb, nbc):
        bs = slice(b0, b0 + nbc)
        for r0 in range(0, tl, rows):
            conv = None
            for w in range(SC_WIDTH):
                s = scp - sch + r0 + w
                term = esc_ref[bs, s:s + rows, :] * cw_ref[w:w + 1, :]
                conv = term if conv is None else conv + term
            c_ref[bs, r0:r0 + rows, :] = gb_ref[bs, r0:r0 + rows, :] * conv
            pos = start + li * tl + r0 + lax.broadcasted_iota(jnp.int32, (nbc, rows, group), 1)
            for gi, win in enumerate(POOL_WINDOWS):
                ls = slice(gi * group, (gi + 1) * group)
                wsum = None
                for i in range(win):
                    s = plp + r0 - i
                    term = epl_ref[bs, s:s + rows, ls]
                    wsum = term if wsum is None else wsum + term
                cnt = jnp.minimum(win, pos + 1).astype(F32)
                pooled_ref[bs, r0:r0 + rows, ls] = wsum / cnt - u_ref[bs, r0:r0 + rows, ls]
    pooled = pooled_ref[...].reshape(nb * tl, c_ref.shape[2]).astype(BF16)
    for gi in range(len(POOL_WINDOWS)):
        ls = slice(gi * group, (gi + 1) * group)
        dg = _dot(pooled[:, ls], pw_ref[gi]) * ps_ref[:, ls]
        d_ref[:, :, ls] = dg.reshape(nb, tl, group)

    @pl.when(li == pl.num_programs(1) - 1)
    def _():
        nsc_ref[...] = esc_ref[:, scp + tl - sch:scp + tl, :]
        npl_ref[...] = epl_ref[:, plp + tl - plh:plp + tl, :]


def _odd_mixer(sc_prev, pool_prev, gb, gcx, u, cw, pw, ps, *, start, nb, tl, nbc, rows):
    bsz, seq, c = gb.shape
    sch, plh = SC_WIDTH - 1, POOL_MAX - 1
    assert seq == tl or tl >= plh
    blk = pl.BlockSpec((nb, tl, c), lambda i, j: (i, j, 0))
    st = lambda h: pl.BlockSpec((nb, h, c), lambda i, j: (i, 0, 0))
    return pl.pallas_call(
        functools.partial(_odd_mixer_kernel, start=start, nb=nb, tl=tl, nbc=nbc, rows=rows),
        grid=(bsz // nb, seq // tl),
        in_specs=[st(sch), st(plh), blk, blk, blk, _const_spec(cw.shape), _const_spec(pw.shape),
                  _const_spec((1, c))],
        out_specs=[blk, blk, st(sch), st(plh)],
        out_shape=[jax.ShapeDtypeStruct((bsz, seq, c), F32), jax.ShapeDtypeStruct((bsz, seq, c), F32),
                   jax.ShapeDtypeStruct((bsz, sch, c), F32), jax.ShapeDtypeStruct((bsz, plh, c), F32)],
        scratch_shapes=[pltpu.VMEM((nb, _round_up(sch, SUBLANES) + tl, c), F32),
                        pltpu.VMEM((nb, _round_up(plh, SUBLANES) + tl, c), F32),
                        pltpu.VMEM((nb, tl, c), F32)],
        compiler_params=_params("parallel", "arbitrary"),
        name="shortconv_pool",
    )(sc_prev, pool_prev, gb, gcx, u, cw, pw, ps)


def _suffix_matrix(n):
    m = (lax.broadcasted_iota(jnp.int32, (n, n), 0) >= lax.broadcasted_iota(jnp.int32, (n, n), 1)).astype(BF16)
    return jnp.concatenate([m, m], axis=0)


def _lane_tiles(x, n):
    return jnp.concatenate([x] * n, axis=1)


def _prompt_attn_kernel(bias_ref, q_ref, kt_ref, vt_ref, o_ref, *, blk):
    qi = pl.program_id(1)
    hpt = LANES // HEAD_DIM
    n_tiles = q_ref.shape[1] // LANES
    heads = range(n_tiles * hpt)
    lane = lax.broadcasted_iota(jnp.int32, (blk, LANES), 1)
    causal = lax.broadcasted_iota(jnp.int32, (blk, blk), 1) < lax.broadcasted_iota(jnp.int32, (blk, blk), 0)
    suffix = _suffix_matrix(blk)
    qms, biases = [], []
    for h in heads:
        a = h % hpt
        tile = slice(h // hpt * LANES, (h // hpt + 1) * LANES)
        in_head = (lane >= a * HEAD_DIM) & (lane < (a + 1) * HEAD_DIM)
        qms.append(jnp.where(in_head, q_ref[:, tile], 0.0).astype(BF16))
        biases.append(bias_ref[h] * LOG2E)

    def block(j, carries, accs, masked):
        cols = pl.ds(pl.multiple_of(j * blk, blk), blk)
        kts = [kt_ref[t * LANES:(t + 1) * LANES, cols] for t in range(n_tiles)]
        vts = [vt_ref[t * LANES:(t + 1) * LANES, cols] for t in range(n_tiles)]
        zs = [_dot(qms[h], kts[h // hpt]) + biases[h] for h in heads]
        sps = [_softplus2(z) for z in zs]
        if masked:
            sps = [jnp.where(causal, sp, 0.0) for sp in sps]
        withins = [_dot(_hi_lo(sp), suffix) for sp in sps]
        ws = [jnp.exp2(z - wi - _lane_tiles(c, blk // LANES)) for z, wi, c in zip(zs, withins, carries)]
        if masked:
            ws = [jnp.where(causal, w, 0.0) for w in ws]
        accs = tuple(acc + _dot_nt(w.astype(BF16), vts[h // hpt]) for h, (acc, w) in enumerate(zip(accs, ws)))
        carries = tuple(c + jnp.broadcast_to(wi[:, :1], (blk, LANES)) for c, wi in zip(carries, withins))
        return carries, accs

    zero = tuple(jnp.zeros((blk, LANES), F32) for _ in heads)
    state = block(qi, zero, zero, True)
    _, accs = lax.fori_loop(0, qi, lambda t, c: block(qi - 1 - t, c[0], c[1], False), state)
    for t in range(n_tiles):
        out = accs[t * hpt]
        for a in range(1, hpt):
            out = jnp.where(lane >= a * HEAD_DIM, accs[t * hpt + a], out)
        o_ref[:, t * LANES:(t + 1) * LANES] = out


def _prompt_attn(bias, q, ktb, vtb):
    bsz, seq, c = q.shape
    blk = ATTN_BLOCK
    assert seq % blk == 0 and c % LANES == 0
    qspec = pl.BlockSpec((None, blk, c), lambda b, i: (b, i, 0))
    kvspec = pl.BlockSpec((None, c, seq), lambda b, i: (b, 0, 0))
    return pl.pallas_call(
        functools.partial(_prompt_attn_kernel, blk=blk),
        grid=(bsz, seq // blk),
        in_specs=[pl.BlockSpec(memory_space=pltpu.SMEM), qspec, kvspec, kvspec],
        out_specs=qspec,
        out_shape=jax.ShapeDtypeStruct((bsz, seq, c), F32),
        compiler_params=_params("parallel", "arbitrary"),
        name="prompt_attention",
    )(bias, q, ktb, vtb)


def _sample_attn_kernel(pt_ref, bias_ref, q_ref, kn_ref, vn_ref, *refs, n_pages, kblk):
    del pt_ref
    k_pages, v_pages, o_ref = refs[:n_pages], refs[n_pages:2 * n_pages], refs[2 * n_pages]
    lq, c = q_ref.shape
    rows = N_HEADS * lq

    row_head = lax.broadcasted_iota(jnp.int32, (rows, c), 0) // lq
    lane_head = lax.broadcasted_iota(jnp.int32, (rows, c), 1) // HEAD_DIM
    qh = jnp.where(row_head == lane_head, jnp.concatenate([q_ref[...]] * N_HEADS, axis=0), 0.0).astype(BF16)
    bias = bias_ref[...]

    pad = jnp.zeros((LANES - lq, c), F32)
    kn = jnp.concatenate([kn_ref[...], pad], axis=0).astype(BF16)
    vn = jnp.concatenate([vn_ref[...], pad], axis=0).astype(BF16)
    q_idx = lax.broadcasted_iota(jnp.int32, (rows, LANES), 0) % lq
    valid = lax.broadcasted_iota(jnp.int32, (rows, LANES), 1) < q_idx
    zn = _dot_nt(qh, kn) + bias
    within_n = _dot(_hi_lo(jnp.where(valid, _softplus2(zn), 0.0)), _suffix_matrix(LANES))
    wn = jnp.where(valid, jnp.exp2(zn - within_n), 0.0)
    out = _dot(wn.astype(BF16), vn)
    carry = jnp.broadcast_to(within_n[:, :1], (rows, LANES))

    kt = jnp.concatenate([r[...].astype(BF16) for r in k_pages], axis=1)
    past = kt.shape[1]
    n_blk = past // kblk
    z = _dot(qh, kt) + _lane_tiles(bias, past // LANES)
    sp = _softplus2(z)
    suffix = _suffix_matrix(kblk)
    withins = [_dot(_hi_lo(sp[:, j * kblk:(j + 1) * kblk]), suffix) for j in range(n_blk)]
    ws = [None] * n_blk
    for j in reversed(range(n_blk)):
        ws[j] = jnp.exp2(z[:, j * kblk:(j + 1) * kblk] - withins[j] - _lane_tiles(carry, kblk // LANES))
        carry = carry + jnp.broadcast_to(withins[j][:, :1], (rows, LANES))
    w = jnp.concatenate(ws, axis=1).astype(BF16)
    vt = jnp.concatenate([r[...].astype(BF16) for r in v_pages], axis=1)
    out = out + _dot_nt(w, vt)

    lane_head = lax.broadcasted_iota(jnp.int32, (lq, c), 1) // HEAD_DIM
    res = jnp.zeros((lq, c), F32)
    for h in range(N_HEADS):
        res = jnp.where(lane_head == h, out[h * lq:(h + 1) * lq, :], res)
    o_ref[...] = res


def _sample_attn(page_table, bias, q, kn, vn, cache_k, cache_v, page_base):
    bsz, lq, c = q.shape
    n_pages = page_table.shape[1]
    page = cache_k.shape[2]
    rows = N_HEADS * lq
    bias_rows = jnp.broadcast_to(jnp.repeat(bias * LOG2E, lq)[:, None], (rows, LANES))
    new = pl.BlockSpec((None, lq, c), lambda b, pt: (b, 0, 0))
    page_spec = lambda p: pl.BlockSpec((None, c, page), lambda b, pt: (page_base + pt[b, p], 0, 0))
    grid_spec = pltpu.PrefetchScalarGridSpec(
        num_scalar_prefetch=1,
        grid=(bsz,),
        in_specs=[pl.BlockSpec((rows, LANES), lambda b, pt: (0, 0)), new, new, new]
        + [page_spec(p) for p in range(n_pages)] * 2,
        out_specs=new,
    )
    return pl.pallas_call(
        functools.partial(_sample_attn_kernel, n_pages=n_pages, kblk=ATTN_BLOCK),
        grid_spec=grid_spec,
        out_shape=jax.ShapeDtypeStruct((bsz, lq, c), F32),
        compiler_params=_params("parallel"),
        name="sample_attention",
    )(page_table, bias_rows, q, kn, vn, *([cache_k] * n_pages), *([cache_v] * n_pages))


def _trunk(x, start, states, attn_fn, w, *, tm, seq_tiles, transposed_kv):
    bsz, seq, d = x.shape
    t = bsz * seq
    h = x.reshape(t, d)
    depth = len(w["w_ffn_in"])
    row = lambda v: v.reshape(1, -1)
    new_conv, new_k, new_v, new_sc, new_pool = [], [], [], [], []
    for layer in range(depth):
        i = layer // 2
        g_mix = row(w["norm_mix_g"][layer])
        if layer % 2 == 0:
            glu, q, *kv = _even_proj(h, g_mix, w["w_in_even"][i], w["w_kv_t"][i], bsz=bsz, tm=tm,
                                     transposed_kv=transposed_kv)
            c = glu.shape[1]
            seq3 = lambda z: z.reshape(bsz, seq, c)
            a, conv_state = _conformer(states["conv"](i), seq3(glu), w["conv_a_w"][i], row(w["conv_a_b"][i]),
                                       row(w["ln_a_g"][i]), row(w["ln_a_b"][i]), **seq_tiles)
            b = attn_fn(i, seq3(q), *kv)
            new_conv.append(conv_state)
            if transposed_kv:
                heads = lambda zt: jnp.transpose(zt.reshape(bsz, N_HEADS, HEAD_DIM, seq), (0, 3, 1, 2))
            else:
                heads = lambda z: z.reshape(bsz, seq, N_HEADS, HEAD_DIM)
            new_k.append(heads(kv[0]))
            new_v.append(heads(kv[1]))
            m1, m2, w_out = a.reshape(t, c), b.reshape(t, c), w["w_out_even"][i]
        else:
            gb, gcx, u = _odd_proj(h, g_mix, w["w_in_odd"][i], tm=tm)
            c = gb.shape[1]
            seq3 = lambda z: z.reshape(bsz, seq, c)
            cc, dd, sc_state, pool_state = _odd_mixer(
                states["sc"](i), states["pool"](i), seq3(gb), seq3(gcx), seq3(u), w["conv_c_w"][i],
                w["pool_w"][i], row(w["pool_scale"][i]), start=start, **seq_tiles)
            new_sc.append(sc_state)
            new_pool.append(pool_state)
            m1, m2, w_out = cc.reshape(t, c), dd.reshape(t, c), w["w_out_odd"][i]
        h = _out_ffn(h, m1, m2, w_out, row(w["norm_ffn_g"][layer]), w["w_ffn_in"][layer], w["w_ffn_out"][layer],
                     row(w["norm_final_g"]), tm=tm, final_norm=layer == depth - 1)
    return (h.reshape(bsz, seq, d), jnp.stack(new_k), jnp.stack(new_v), jnp.stack(new_conv),
            jnp.stack(new_sc), jnp.stack(new_pool))


def kernel(x_prompt, x_sample, cache_k, cache_v, state_conformer, state_shortconv, state_pool, page_table,
           norm_mix_g, norm_ffn_g, norm_final_g, w_in_even, conv_a_w, conv_a_b, ln_a_g, ln_a_b, sb_bias,
           w_out_even, w_in_odd, conv_c_w, pool_w, pool_scale, w_out_odd, w_ffn_in, w_ffn_out):
    layers = lambda a: [a[l].astype(BF16) for l in range(a.shape[0])]
    kv0 = w_in_even.shape[2] // 5 * 3
    w = dict(norm_mix_g=norm_mix_g, norm_ffn_g=norm_ffn_g, norm_final_g=norm_final_g,
             w_in_even=layers(w_in_even), w_kv_t=[wl[:, kv0:].T for wl in layers(w_in_even)],
             conv_a_w=conv_a_w, conv_a_b=conv_a_b, ln_a_g=ln_a_g, ln_a_b=ln_a_b,
             w_out_even=layers(w_out_even), w_in_odd=layers(w_in_odd), conv_c_w=conv_c_w,
             pool_w=layers(pool_w), pool_scale=pool_scale, w_out_odd=layers(w_out_odd),
             w_ffn_in=layers(w_ffn_in), w_ffn_out=layers(w_ffn_out))
    bp, dt = x_prompt.shape[0], x_prompt.dtype
    conv_dim, sc_dim, pool_dim = state_conformer.shape[-1], state_shortconv.shape[-1], state_pool.shape[-1]

    prompt_states = dict(conv=lambda i: jnp.zeros((bp, CONV_WIDTH - 1, conv_dim), dt),
                         sc=lambda i: jnp.zeros((bp, SC_WIDTH - 1, sc_dim), dt),
                         pool=lambda i: jnp.zeros((bp, POOL_MAX - 1, pool_dim), dt))
    prompt_attn = lambda i, q, kt, vt, ktb, vtb: _prompt_attn(sb_bias[i], q, ktb, vtb)
    y_p, k_p, v_p, conv_p, sc_p, pool_p = _trunk(
        x_prompt, 0, prompt_states, prompt_attn, w, tm=512, seq_tiles=dict(nb=1, tl=256, nbc=1, rows=32),
        transposed_kv=True)

    n_pool, page = cache_k.shape[1], cache_k.shape[2]
    past_len = page_table.shape[1] * page
    pages = lambda cache: jnp.transpose(cache, (0, 1, 3, 4, 2)).reshape(
        cache.shape[0] * n_pool, N_HEADS * HEAD_DIM, page)
    sample_states = dict(conv=lambda i: state_conformer[i], sc=lambda i: state_shortconv[i],
                         pool=lambda i: state_pool[i])
    bs, ls = x_sample.shape[0], x_sample.shape[1]
    seq3 = lambda z: z.reshape(bs, ls, -1)
    sample_attn = lambda i, q, k, v: _sample_attn(
        page_table, sb_bias[i], q, seq3(k), seq3(v), pages(cache_k), pages(cache_v), i * n_pool)
    y_s, k_s, v_s, conv_s, sc_s, pool_s = _trunk(
        x_sample, past_len, sample_states, sample_attn, w, tm=512,
        seq_tiles=dict(nb=16, tl=ls, nbc=4, rows=ls), transposed_kv=False)

    return (y_p, y_s, k_p, v_p, k_s, v_s, conv_p, conv_s, sc_p, sc_s, pool_p, pool_s)
```
